```python
import functools
import jax, jax.numpy as jnp
from jax import lax
import numpy as np

D_MODEL = 1024
BATCH = 4
SEQ = 4096
DEPTH = 4
DEC_BATCH = 32
DEC_SEQ = 64
PAST_LEN = 1024

CHUNK = 64
N_META = 16
HEAD_DIM = 64
N_Q_HEADS = 16
N_KV_HEADS = 2
GROUP = N_Q_HEADS // N_KV_HEADS
ROT_DIM = HEAD_DIM // 4
ROPE_THETA = 500000.0
WINDOW = 128
WIN_CHUNKS = WINDOW // CHUNK
ATTN_SCALE = HEAD_DIM ** -0.5
SC_WIDTH = D_MODEL
SC_K = 3
CF_WIDTH = D_MODEL
CF_K = 31
D_FF = 2816
FFN_K = 3
NORM_EPS = 1e-5
NEG_BIG = -1e30
Q_W = N_Q_HEADS * HEAD_DIM
KV_W = N_KV_HEADS * HEAD_DIM
IN_SIZES = (Q_W, KV_W, KV_W, SC_WIDTH, SC_WIDTH, SC_WIDTH, CF_WIDTH, CF_WIDTH, 3 * D_MODEL)
N_IN = sum(IN_SIZES)
IN_SPLITS = tuple(int(c) for c in np.cumsum(IN_SIZES)[:-1])

kernel_name = 'hybrid_streaming_swa_shortconv_conformer_step'


def rmsnorm(x, g):
    xf = x.astype(jnp.float32)
    y = xf * lax.rsqrt(jnp.mean(xf * xf, axis=-1, keepdims=True) + NORM_EPS)
    return (y * g.astype(jnp.float32)).astype(x.dtype)


def layernorm(x, g, b):
    xf = x.astype(jnp.float32)
    mu = jnp.mean(xf, axis=-1, keepdims=True)
    xc = xf - mu
    y = xc * lax.rsqrt(jnp.mean(xc * xc, axis=-1, keepdims=True) + NORM_EPS)
    return (y * g.astype(jnp.float32) + b.astype(jnp.float32)).astype(x.dtype)


def partial_rotary(x, pos):
    half = ROT_DIM // 2
    inv_freq = jnp.power(ROPE_THETA, -jnp.arange(0, ROT_DIM, 2, dtype=jnp.float32) / ROT_DIM)
    ang = pos.astype(jnp.float32)[:, None] * inv_freq[None, :]
    cos = jnp.cos(ang)[None, :, None, :].astype(x.dtype)
    sin = jnp.sin(ang)[None, :, None, :].astype(x.dtype)
    x1, x2 = x[..., :half], x[..., half:ROT_DIM]
    return jnp.concatenate([x1 * cos - x2 * sin, x2 * cos + x1 * sin, x[..., ROT_DIM:]], axis=-1)


def sink_softmax(s, sink):
    m = jnp.maximum(jnp.max(s, axis=-1, keepdims=True), sink)
    e = jnp.exp(s - m)
    return e / (jnp.sum(e, axis=-1, keepdims=True) + jnp.exp(sink - m))


def causal_dwconv(x, hist, w):
    xh = jnp.concatenate([hist.astype(x.dtype), x], axis=1)
    y = lax.conv_general_dilated(xh, w[:, None, :].astype(x.dtype), window_strides=(1,), padding='VALID',
                                 dimension_numbers=('NWC', 'WIO', 'NWC'), feature_group_count=x.shape[-1])
    return y, xh[:, xh.shape[1] - (w.shape[0] - 1):]


def swa_prompt(q, k, v, sinks):
    n, l = q.shape[:2]
    lead = CHUNK - N_META
    nb = (l + lead) // CHUNK
    front = lead + WIN_CHUNKS * CHUNK
    qb = jnp.pad(q, ((0, 0), (lead, 0), (0, 0), (0, 0))).reshape(n, nb, CHUNK, N_KV_HEADS, GROUP, HEAD_DIM)

    def band(a):
        ab = jnp.pad(a, ((0, 0), (front, 0), (0, 0), (0, 0))).reshape(n, nb + WIN_CHUNKS, CHUNK, N_KV_HEADS, HEAD_DIM)
        return jnp.concatenate([ab[:, i:i + nb] for i in range(WIN_CHUNKS + 1)], axis=2)

    kb, vb = band(k), band(v)
    valid = (jnp.arange((nb + WIN_CHUNKS) * CHUNK) >= front).reshape(nb + WIN_CHUNKS, CHUNK)
    valid = jnp.concatenate([valid[i:i + nb] for i in range(WIN_CHUNKS + 1)], axis=1)
    s = jnp.einsum('nbqhgd,nbkhd->nbhgqk', qb, kb).astype(jnp.float32) * ATTN_SCALE
    s = jnp.where(valid[None, :, None, None, None, :], s, NEG_BIG)
    sink = sinks.astype(jnp.float32).reshape(N_KV_HEADS, GROUP)[None, None, :, :, None, None]
    p = sink_softmax(s, sink)
    o = jnp.einsum('nbhgqk,nbkhd->nbqhgd', p.astype(v.dtype), vb)
    o = o.reshape(n, nb * CHUNK, Q_W)[:, lead:]
    return o, k[:, l - WINDOW:], v[:, l - WINDOW:]


def swa_sample(q, k, v, sinks, ck, cv):
    n, s_len = q.shape[:2]
    kc = jnp.concatenate([ck.astype(k.dtype), k], axis=1)
    vc = jnp.concatenate([cv.astype(v.dtype), v], axis=1)
    qg = q.reshape(n, s_len, N_KV_HEADS, GROUP, HEAD_DIM)
    s = jnp.einsum('nqhgd,nkhd->nhgqk', qg, kc).astype(jnp.float32) * ATTN_SCALE
    sink = sinks.astype(jnp.float32).reshape(N_KV_HEADS, GROUP)[None, :, :, None, None]
    p = sink_softmax(s, sink)
    o = jnp.einsum('nhgqk,nkhd->nqhgd', p.astype(vc.dtype), vc).reshape(n, s_len, Q_W)
    w = ck.shape[1]
    return o, kc[:, kc.shape[1] - w:], vc[:, vc.shape[1] - w:]


def trunk_layer(x, pos, attend, sc_hist, cf_hist, ffn_hist, p):
    n, l = x.shape[:2]
    xn = rmsnorm(x, p['norm1_g'])
    proj = xn @ p['w_in']
    q, k, v, sc_b, sc_c, sc_h, cf_a, cf_gate, gate_pre = jnp.split(proj, IN_SPLITS, axis=-1)
    q = partial_rotary(q.reshape(n, l, N_Q_HEADS, HEAD_DIM), pos)
    k = partial_rotary(k.reshape(n, l, N_KV_HEADS, HEAD_DIM), pos)
    v = v.reshape(n, l, N_KV_HEADS, HEAD_DIM)
    y_attn, new_k, new_v = attend(q, k, v, p['attn_sinks'])
    y_attn = y_attn @ p['w_attn_out']
    sc_conv, new_sc = causal_dwconv(sc_c * sc_h, sc_hist, p['sc_conv_w'])
    y_sc = (sc_b * sc_conv) @ p['w_sc_out']
    glu = cf_a * jax.nn.sigmoid(cf_gate)
    cf_conv, new_cf = causal_dwconv(glu, cf_hist, p['cf_conv_w'])
    y_cf = jax.nn.silu(layernorm(cf_conv, p['cf_ln_g'], p['cf_ln_b'])) @ p['w_cf_out']
    g_attn, g_sc, g_cf = jnp.split(jax.nn.sigmoid(gate_pre + p['b_gate']), 3, axis=-1)
    x = x + (g_attn * y_attn + g_sc * y_sc + g_cf * y_cf) @ p['w_o']
    h = rmsnorm(x, p['norm2_g']) @ p['w_up']
    h, new_ffn = causal_dwconv(h, ffn_hist, p['ffn_conv_w'])
    hg, hu = jnp.split(h, 2, axis=-1)
    x = x + (jax.nn.silu(hg) * hu) @ p['w_down']
    return x, (new_k, new_v, new_sc, new_cf, new_ffn)


def setup_inputs(seed: int = 0) -> dict:
    key = jax.random.key(seed)
    ks = jax.random.split(key, 32)
    f32 = jnp.float32
    nrm = lambda k, shape, s: jax.random.normal(k, shape, f32) * s
    swa_cache = min(WINDOW, PAST_LEN)
    return {
        'x_prompt': nrm(ks[0], (BATCH, SEQ, D_MODEL), 1.0),
        'x_sample': nrm(ks[1], (DEC_BATCH, DEC_SEQ, D_MODEL), 1.0),
        'cache_k': nrm(ks[2], (DEPTH, DEC_BATCH, swa_cache, N_KV_HEADS, HEAD_DIM), 1.0),
        'cache_v': nrm(ks[3], (DEPTH, DEC_BATCH, swa_cache, N_KV_HEADS, HEAD_DIM), 1.0),
        'state_sc': nrm(ks[4], (DEPTH, DEC_BATCH, SC_K - 1, SC_WIDTH), 0.5),
        'state_cf': nrm(ks[5], (DEPTH, DEC_BATCH, CF_K - 1, CF_WIDTH), 0.5),
        'state_ffn': nrm(ks[6], (DEPTH, DEC_BATCH, FFN_K - 1, 2 * D_FF), 1.0),
        'meta_tokens': nrm(ks[7], (N_META, D_MODEL), 1.0),
        'norm1_g': 1.0 + nrm(ks[8], (DEPTH, D_MODEL), 0.02),
        'w_in': nrm(ks[9], (DEPTH, D_MODEL, N_IN), D_MODEL ** -0.5),
        'b_gate': nrm(ks[10], (DEPTH, 3 * D_MODEL), 0.02),
        'attn_sinks': nrm(ks[11], (DEPTH, N_Q_HEADS), 0.5),
        'w_attn_out': nrm(ks[12], (DEPTH, Q_W, D_MODEL), Q_W ** -0.5),
        'sc_conv_w': nrm(ks[13], (DEPTH, SC_K, SC_WIDTH), SC_K ** -0.5),
        'w_sc_out': nrm(ks[14], (DEPTH, SC_WIDTH, D_MODEL), SC_WIDTH ** -0.5),
        'cf_conv_w': nrm(ks[15], (DEPTH, CF_K, CF_WIDTH), CF_K ** -0.5),
        'cf_ln_g': 1.0 + nrm(ks[16], (DEPTH, CF_WIDTH), 0.02),
        'cf_ln_b': nrm(ks[17], (DEPTH, CF_WIDTH), 0.02),
        'w_cf_out': nrm(ks[18], (DEPTH, CF_WIDTH, D_MODEL), CF_WIDTH ** -0.5),
        'w_o': nrm(ks[19], (DEPTH, D_MODEL, D_MODEL), D_MODEL ** -0.5),
        'norm2_g': 1.0 + nrm(ks[20], (DEPTH, D_MODEL), 0.02),
        'w_up': nrm(ks[21], (DEPTH, D_MODEL, 2 * D_FF), D_MODEL ** -0.5),
        'ffn_conv_w': nrm(ks[22], (DEPTH, FFN_K, 2 * D_FF), FFN_K ** -0.5),
        'w_down': nrm(ks[23], (DEPTH, D_FF, D_MODEL), D_FF ** -0.5),
        'final_g': 1.0 + nrm(ks[24], (D_MODEL,), 0.02),
    }


def reference(x_prompt, x_sample, cache_k, cache_v, state_sc, state_cf, state_ffn, meta_tokens, norm1_g, w_in,
              b_gate, attn_sinks, w_attn_out, sc_conv_w, w_sc_out, cf_conv_w, cf_ln_g, cf_ln_b, w_cf_out, w_o,
              norm2_g, w_up, ffn_conv_w, w_down, final_g):
    b = x_prompt.shape[0]
    ds = x_sample.shape[1]
    dt = x_prompt.dtype
    xp = jnp.concatenate([jnp.broadcast_to(meta_tokens.astype(dt)[None], (b, N_META, D_MODEL)), x_prompt], axis=1)
    pos_p = jnp.arange(xp.shape[1], dtype=jnp.int32)
    xs = x_sample
    pos_s = PAST_LEN + jnp.arange(ds, dtype=jnp.int32)
    zero_sc = jnp.zeros((b, SC_K - 1, SC_WIDTH), dt)
    zero_cf = jnp.zeros((b, CF_K - 1, CF_WIDTH), dt)
    zero_ffn = jnp.zeros((b, FFN_K - 1, 2 * D_FF), dt)
    st_p, st_s = [], []
    for i in range(DEPTH):
        p = dict(norm1_g=norm1_g[i], w_in=w_in[i], b_gate=b_gate[i], attn_sinks=attn_sinks[i],
                 w_attn_out=w_attn_out[i], sc_conv_w=sc_conv_w[i], w_sc_out=w_sc_out[i], cf_conv_w=cf_conv_w[i],
                 cf_ln_g=cf_ln_g[i], cf_ln_b=cf_ln_b[i], w_cf_out=w_cf_out[i], w_o=w_o[i], norm2_g=norm2_g[i],
                 w_up=w_up[i], ffn_conv_w=ffn_conv_w[i], w_down=w_down[i])
        xp, sp = trunk_layer(xp, pos_p, swa_prompt, zero_sc, zero_cf, zero_ffn, p)
        attend_s = functools.partial(swa_sample, ck=cache_k[i], cv=cache_v[i])
        xs, ss = trunk_layer(xs, pos_s, attend_s, state_sc[i], state_cf[i], state_ffn[i], p)
        st_p.append(sp)
        st_s.append(ss)
    y_prompt = rmsnorm(xp, final_g)[:, N_META:]
    y_sample = rmsnorm(xs, final_g)
    k_prompt = jnp.stack([s[0] for s in st_p])
    v_prompt = jnp.stack([s[1] for s in st_p])
    sc_prompt = jnp.stack([s[2] for s in st_p])
    cf_prompt = jnp.stack([s[3] for s in st_p])
    ffn_prompt = jnp.stack([s[4] for s in st_p])
    k_sample = jnp.stack([s[0] for s in st_s])
    v_sample = jnp.stack([s[1] for s in st_s])
    sc_sample = jnp.stack([s[2] for s in st_s])
    cf_sample = jnp.stack([s[3] for s in st_s])
    ffn_sample = jnp.stack([s[4] for s in st_s])
    return (y_prompt, y_sample, k_prompt, v_prompt, sc_prompt, cf_prompt, ffn_prompt,
            k_sample, v_sample, sc_sample, cf_sample, ffn_sample)
```

```python
import functools

import jax
import jax.numpy as jnp
from jax import lax
from jax.experimental import pallas as pl
from jax.experimental.pallas import tpu as pltpu

D_MODEL = 1024
CHUNK = 64
N_META = 16
HEAD_DIM = 64
N_Q_HEADS = 16
N_KV_HEADS = 2
GROUP = N_Q_HEADS // N_KV_HEADS
ROT_DIM = HEAD_DIM // 4
ROPE_THETA = 500000.0
WINDOW = 128
ATTN_SCALE = HEAD_DIM ** -0.5
SC_K = 3
CF_K = 31
D_FF = 2816
FFN_K = 3
NORM_EPS = 1e-5
NEG_BIG = -1e30
PAST_LEN = 1024
Q_W = N_Q_HEADS * HEAD_DIM
KV_W = N_KV_HEADS * HEAD_DIM
N_IN = Q_W + 2 * KV_W + 3 * D_MODEL + 2 * D_MODEL + 3 * D_MODEL
OFF_Q = 0
OFF_SCB = Q_W + 2 * KV_W
OFF_SCC = OFF_SCB + D_MODEL
OFF_SCH = OFF_SCC + D_MODEL
OFF_CFA = OFF_SCH + D_MODEL
OFF_CFG = OFF_CFA + D_MODEL
OFF_GATE = OFF_CFG + D_MODEL

LANES = 128
SUBLANES = 8
VMEM_LIMIT_BYTES = 58 * 1024 * 1024

PROMPT_TILE_ROWS = 5 * CHUNK
SAMPLE_SEQS_PER_TILE = 4
COL_BLOCK = 512
FFN_COL_BLOCK = 256
CF_HIST_PAD = 32
SC_HIST_PAD = SUBLANES

F32 = jnp.float32
BF16 = jnp.bfloat16


def _dot(a, b):
    return jnp.dot(a, b, preferred_element_type=F32)


def _rmsnorm_rows(x, g):
    ms = jnp.mean(x * x, axis=-1, keepdims=True)
    return (x * lax.rsqrt(ms + NORM_EPS)) * g


def _kv_variants(k):
    lane = lax.broadcasted_iota(jnp.int32, k.shape, 1)
    lo = lane < HEAD_DIM
    kr = pltpu.roll(k, HEAD_DIM, 1)
    z = jnp.zeros_like(k)
    out = (jnp.where(lo, k, z), jnp.where(lo, z, kr), jnp.where(lo, kr, z), jnp.where(lo, z, k))
    return tuple(o.astype(BF16) for o in out)


def _mixer_kernel(sinks_ref, x_ref, cos_ref, sa_ref, sb_ref, kst_ref, vst_ref, scst_ref, cfst_ref,
                  n1g_ref, win_ref, bg_ref, wao_ref, scw_ref, wso_ref, cfw_ref, lng_ref, lnb_ref, wco_ref, wo_ref,
                  xmid_ref, knew_ref, vnew_ref, scnew_ref, cfnew_ref,
                  xn_s, q_s, kb_s, vb_s, ubuf, gbuf, act_s, conv_s, merged_s,
                  *, G, R, NT, lead):
    M = G * R
    nch = R // CHUNK
    t = pl.program_id(1)
    is_first = t == 0

    def load_state():
        for g in range(G):
            for j, kv in enumerate(_kv_variants(kst_ref[g])):
                kb_s[j, g, 0:WINDOW, :] = kv
            for j, vv in enumerate(_kv_variants(vst_ref[g])):
                vb_s[j, g, 0:WINDOW, :] = vv
            ubuf[g, SC_HIST_PAD - (SC_K - 1):SC_HIST_PAD, :] = scst_ref[g]
            gbuf[g, CF_HIST_PAD - (CF_K - 1):CF_HIST_PAD, :] = cfst_ref[g]

    def carry_state():
        for g in range(G):
            for j in range(4):
                kb_s[j, g, 0:WINDOW, :] = kb_s[j, g, R:R + WINDOW, :]
                vb_s[j, g, 0:WINDOW, :] = vb_s[j, g, R:R + WINDOW, :]
            ubuf[g, SC_HIST_PAD - (SC_K - 1):SC_HIST_PAD, :] = ubuf[g, R + SC_HIST_PAD - (SC_K - 1):R + SC_HIST_PAD, :]
            gbuf[g, CF_HIST_PAD - (CF_K - 1):CF_HIST_PAD, :] = gbuf[g, R + CF_HIST_PAD - (CF_K - 1):R + CF_HIST_PAD, :]

    if NT == 1:
        load_state()
    else:
        pl.when(is_first)(load_state)
        pl.when(jnp.logical_not(is_first))(carry_state)

    x = x_ref[...].reshape(M, D_MODEL)
    xn_s[...] = _rmsnorm_rows(x, n1g_ref[...]).astype(BF16)
    xnb = xn_s[...]

    def rows_tiled(ref):
        v = ref[...]
        return v if G == 1 else jnp.concatenate([v] * G, axis=0)

    cos_t, sa_t, sb_t = rows_tiled(cos_ref), rows_tiled(sa_ref), rows_tiled(sb_ref)

    def rotary(p):
        return p * cos_t + pltpu.roll(p, ROT_DIM // 2, 1) * sa_t + pltpu.roll(p, LANES - ROT_DIM // 2, 1) * sb_t

    k_rot = None
    v_new = None
    for j in range((Q_W + 2 * KV_W) // (2 * LANES)):
        pj = _dot(xnb, win_ref[:, 2 * LANES * j:2 * LANES * (j + 1)])
        for half in range(2):
            blk = 2 * j + half
            p = pj[:, LANES * half:LANES * (half + 1)]
            if blk < Q_W // LANES:
                q_s[:, LANES * blk:LANES * (blk + 1)] = (rotary(p) * ATTN_SCALE).astype(BF16)
            elif blk == Q_W // LANES:
                k_rot = rotary(p)
            else:
                v_new = p

    for g in range(G):
        kg = k_rot[g * R:(g + 1) * R]
        vg = v_new[g * R:(g + 1) * R]
        for j, kv in enumerate(_kv_variants(kg)):
            kb_s[j, g, WINDOW:WINDOW + R, :] = kv
        for j, vv in enumerate(_kv_variants(vg)):
            vb_s[j, g, WINDOW:WINDOW + R, :] = vv
        if R >= WINDOW:
            knew_ref[g] = kg[R - WINDOW:]
            vnew_ref[g] = vg[R - WINDOW:]
        else:
            knew_ref[g] = jnp.concatenate([kst_ref[g, R:WINDOW, :], kg], axis=0)
            vnew_ref[g] = jnp.concatenate([vst_ref[g, R:WINDOW, :], vg], axis=0)

    band = 3 * CHUNK
    rblk = lax.broadcasted_iota(jnp.int32, (4 * CHUNK, 1), 0) // CHUNK
    sink_cols = []
    for h in range(N_KV_HEADS):
        for a in range(2):
            col = jnp.zeros((4 * CHUNK, 1), F32)
            for p in range(4):
                col = jnp.where(rblk == p, sinks_ref[GROUP * h + 2 * p + a], col)
            sink_cols.append(col)
    if lead is not None:
        first_invalid = jnp.where(is_first, WINDOW + lead, 0)
        kcol = lax.broadcasted_iota(jnp.int32, (1, band), 1)

    def attn_body(idx, carry):
        if G == 1:
            g, c = 0, idx
        elif nch == 1:
            g, c = idx, 0
        else:
            g, c = idx // nch, idx % nch
        row0 = pl.multiple_of(g * R + c * CHUNK, CHUNK)
        b0 = pl.multiple_of(c * CHUNK, CHUNK)
        for h in range(N_KV_HEADS):
            qst = jnp.concatenate(
                [q_s[pl.ds(row0, CHUNK), (4 * h + p) * LANES:(4 * h + p + 1) * LANES] for p in range(4)], axis=0)
            o = jnp.zeros((4 * CHUNK, LANES), F32)
            for a in range(2):
                kk = kb_s[2 * h + a, g, pl.ds(b0, band), :]
                s = lax.dot_general(qst, kk, (((1,), (1,)), ((), ())), preferred_element_type=F32)
                if lead is not None:
                    s = jnp.where(kcol + b0 >= first_invalid, s, NEG_BIG)
                sink = sink_cols[2 * h + a]
                m = jnp.maximum(jnp.max(s, axis=-1, keepdims=True), sink)
                e = jnp.exp(s - m)
                den = jnp.sum(e, axis=-1, keepdims=True) + jnp.exp(sink - m)
                vv = vb_s[2 * h + a, g, pl.ds(b0, band), :]
                o = o + _dot(e.astype(BF16), vv) * (1.0 / den)
            for p in range(4):
                act_s[pl.ds(row0, CHUNK), (4 * h + p) * LANES:(4 * h + p + 1) * LANES] = (
                    o[p * CHUNK:(p + 1) * CHUNK].astype(BF16))
        return carry

    lax.fori_loop(0, G * nch, attn_body, 0)

    ncb = D_MODEL // COL_BLOCK

    def gate(branch, cs):
        pre = _dot(xnb, win_ref[:, OFF_GATE + branch * D_MODEL + cs.start:OFF_GATE + branch * D_MODEL + cs.stop])
        return jax.nn.sigmoid(pre + bg_ref[:, branch * D_MODEL + cs.start:branch * D_MODEL + cs.stop])

    for j in range(ncb):
        cs = slice(j * COL_BLOCK, (j + 1) * COL_BLOCK)
        merged_s[:, cs] = gate(0, cs) * _dot(act_s[...], wao_ref[:, cs])

    for j in range(ncb):
        cs = slice(j * COL_BLOCK, (j + 1) * COL_BLOCK)
        sc_b = _dot(xnb, win_ref[:, OFF_SCB + cs.start:OFF_SCB + cs.stop])
        u = (_dot(xnb, win_ref[:, OFF_SCC + cs.start:OFF_SCC + cs.stop])
             * _dot(xnb, win_ref[:, OFF_SCH + cs.start:OFF_SCH + cs.stop]))
        for g in range(G):
            ubuf[g, SC_HIST_PAD:SC_HIST_PAD + R, cs] = u[g * R:(g + 1) * R]
        conv = []
        for g in range(G):
            acc = scw_ref[SC_K - 1:SC_K, cs] * u[g * R:(g + 1) * R]
            for k in range(SC_K - 1):
                o0 = SC_HIST_PAD - (SC_K - 1) + k
                acc = acc + scw_ref[k:k + 1, cs] * ubuf[g, o0:o0 + R, cs]
            conv.append(acc)
        conv = conv[0] if G == 1 else jnp.concatenate(conv, axis=0)
        act_s[:, cs] = (sc_b * conv).astype(BF16)
    for g in range(G):
        scnew_ref[g] = ubuf[g, R + SC_HIST_PAD - (SC_K - 1):R + SC_HIST_PAD, :]
    for j in range(ncb):
        cs = slice(j * COL_BLOCK, (j + 1) * COL_BLOCK)
        merged_s[:, cs] = merged_s[:, cs] + gate(1, cs) * _dot(act_s[...], wso_ref[:, cs])

    for j in range(ncb):
        cs = slice(j * COL_BLOCK, (j + 1) * COL_BLOCK)
        glu = (_dot(xnb, win_ref[:, OFF_CFA + cs.start:OFF_CFA + cs.stop])
               * jax.nn.sigmoid(_dot(xnb, win_ref[:, OFF_CFG + cs.start:OFF_CFG + cs.stop])))
        for g in range(G):
            gbuf[g, CF_HIST_PAD:CF_HIST_PAD + R, cs] = glu[g * R:(g + 1) * R]
    for g in range(G):
        cfnew_ref[g] = gbuf[g, R + CF_HIST_PAD - (CF_K - 1):R + CF_HIST_PAD, :]

    def conv31_body(cb, carry):
        lanes = pl.ds(pl.multiple_of(cb * LANES, LANES), LANES)
        for g in range(G):
            for rb in range(nch):
                acc = jnp.zeros((CHUNK, LANES), F32)
                for k in range(CF_K):
                    o0 = CF_HIST_PAD - (CF_K - 1) + k + rb * CHUNK
                    acc = acc + cfw_ref[k:k + 1, lanes] * gbuf[g, o0:o0 + CHUNK, lanes]
                conv_s[g * R + rb * CHUNK:g * R + (rb + 1) * CHUNK, lanes] = acc
        return carry

    lax.fori_loop(0, D_MODEL // LANES, conv31_body, 0)

    cv = conv_s[...]
    mu = jnp.mean(cv, axis=-1, keepdims=True)
    xc = cv - mu
    var = jnp.mean(xc * xc, axis=-1, keepdims=True)
    y = xc * lax.rsqrt(var + NORM_EPS) * lng_ref[...] + lnb_ref[...]
    act_s[...] = (y * jax.nn.sigmoid(y)).astype(BF16)
    for j in range(ncb):
        cs = slice(j * COL_BLOCK, (j + 1) * COL_BLOCK)
        merged_s[:, cs] = merged_s[:, cs] + gate(2, cs) * _dot(act_s[...], wco_ref[:, cs])

    out = x + _dot(merged_s[...].astype(BF16), wo_ref[...])
    if lead is not None:
        row = lax.broadcasted_iota(jnp.int32, (M, 1), 0)
        out = jnp.where(jnp.logical_or(jnp.logical_not(is_first), row >= lead), out, 0.0)
    xmid_ref[...] = out.reshape(G, R, D_MODEL)


def _ffn_kernel(x_ref, fst_ref, n2g_ref, wup_ref, fcw_ref, wdn_ref, fing_ref,
                xout_ref, fnew_ref,
                xn_s, hbuf, act_s,
                *, G, R, NT, final_norm):
    M = G * R
    t = pl.program_id(1)
    is_first = t == 0
    h0 = SC_HIST_PAD - (FFN_K - 1)

    def load_state():
        for g in range(G):
            hbuf[g, h0:SC_HIST_PAD, :] = fst_ref[g]

    def carry_state():
        for g in range(G):
            hbuf[g, h0:SC_HIST_PAD, :] = hbuf[g, R + h0:R + SC_HIST_PAD, :]

    if NT == 1:
        load_state()
    else:
        pl.when(is_first)(load_state)
        pl.when(jnp.logical_not(is_first))(carry_state)

    x = x_ref[...].reshape(M, D_MODEL)
    xn_s[...] = _rmsnorm_rows(x, n2g_ref[...]).astype(BF16)
    xnb = xn_s[...]

    def conv3(cs):
        h = _dot(xnb, wup_ref[:, cs])
        out = []
        for g in range(G):
            hg = h[g * R:(g + 1) * R]
            hbuf[g, SC_HIST_PAD:SC_HIST_PAD + R, cs] = hg
            acc = fcw_ref[FFN_K - 1:FFN_K, cs] * hg
            for k in range(FFN_K - 1):
                acc = acc + fcw_ref[k:k + 1, cs] * hbuf[g, h0 + k:h0 + k + R, cs]
            out.append(acc)
        return out[0] if G == 1 else jnp.concatenate(out, axis=0)

    for j in range(D_FF // FFN_COL_BLOCK):
        cg = slice(j * FFN_COL_BLOCK, (j + 1) * FFN_COL_BLOCK)
        cu = slice(D_FF + j * FFN_COL_BLOCK, D_FF + (j + 1) * FFN_COL_BLOCK)
        hg = conv3(cg)
        hu = conv3(cu)
        act_s[:, cg] = (hg * jax.nn.sigmoid(hg) * hu).astype(BF16)
    for g in range(G):
        fnew_ref[g] = hbuf[g, R + h0:R + SC_HIST_PAD, :]

    out = x + _dot(act_s[...], wdn_ref[...])
    if final_norm:
        out = _rmsnorm_rows(out, fing_ref[...])
    xout_ref[...] = out.reshape(G, R, D_MODEL)


def _const_spec(shape, layer):
    nd = len(shape)
    if layer is None:
        return pl.BlockSpec(shape, lambda n, t: (0,) * nd, pipeline_mode=pl.Buffered(1))
    return pl.BlockSpec((None,) + shape, lambda n, t: (layer,) + (0,) * nd, pipeline_mode=pl.Buffered(1))


def _mixer_call(layer, x, tabs, kst, vst, scst, cfst, p, *, G, R, lead):
    N, L, _ = x.shape
    NT = L // R
    M = G * R
    grid = (N // G, NT)
    row_spec = pl.BlockSpec((G, R, D_MODEL), lambda n, t: (n, t, 0))
    tab_spec = pl.BlockSpec((R, LANES), lambda n, t: (t, 0))

    def st_spec(rows, width):
        return pl.BlockSpec((G, rows, width), lambda n, t: (n, 0, 0))

    in_specs = [
        pl.BlockSpec(memory_space=pltpu.SMEM),
        row_spec, tab_spec, tab_spec, tab_spec,
        st_spec(WINDOW, KV_W), st_spec(WINDOW, KV_W), st_spec(SC_K - 1, D_MODEL), st_spec(CF_K - 1, D_MODEL),
        _const_spec((1, D_MODEL), layer), _const_spec((D_MODEL, N_IN), layer), _const_spec((1, 3 * D_MODEL), layer),
        _const_spec((Q_W, D_MODEL), layer), _const_spec((SC_K, D_MODEL), layer), _const_spec((D_MODEL, D_MODEL), layer),
        _const_spec((CF_K, D_MODEL), layer), _const_spec((1, D_MODEL), layer), _const_spec((1, D_MODEL), layer),
        _const_spec((D_MODEL, D_MODEL), layer), _const_spec((D_MODEL, D_MODEL), layer),
    ]
    out_specs = [row_spec, st_spec(WINDOW, KV_W), st_spec(WINDOW, KV_W), st_spec(SC_K - 1, D_MODEL),
                 st_spec(CF_K - 1, D_MODEL)]
    out_shape = [
        jax.ShapeDtypeStruct((N, L, D_MODEL), F32),
        jax.ShapeDtypeStruct((N, WINDOW, KV_W), F32), jax.ShapeDtypeStruct((N, WINDOW, KV_W), F32),
        jax.ShapeDtypeStruct((N, SC_K - 1, D_MODEL), F32), jax.ShapeDtypeStruct((N, CF_K - 1, D_MODEL), F32),
    ]
    scratch = [
        pltpu.VMEM((M, D_MODEL), BF16),
        pltpu.VMEM((M, Q_W), BF16),
        pltpu.VMEM((4, G, WINDOW + R, LANES), BF16),
        pltpu.VMEM((4, G, WINDOW + R, LANES), BF16),
        pltpu.VMEM((G, SC_HIST_PAD + R, D_MODEL), F32),
        pltpu.VMEM((G, CF_HIST_PAD + R, D_MODEL), F32),
        pltpu.VMEM((M, D_MODEL), BF16),
        pltpu.VMEM((M, D_MODEL), F32),
        pltpu.VMEM((M, D_MODEL), F32),
    ]
    kern = functools.partial(_mixer_kernel, G=G, R=R, NT=NT, lead=lead)
    return pl.pallas_call(
        kern, grid=grid, in_specs=in_specs, out_specs=out_specs, out_shape=out_shape, scratch_shapes=scratch,
        compiler_params=pltpu.CompilerParams(dimension_semantics=("arbitrary", "arbitrary"),
                                             vmem_limit_bytes=VMEM_LIMIT_BYTES),
        name="mixer",
    )(p["attn_sinks"][layer], x, *tabs, kst, vst, scst, cfst,
      p["norm1_g"], p["w_in"], p["b_gate"], p["w_attn_out"], p["sc_conv_w"], p["w_sc_out"],
      p["cf_conv_w"], p["cf_ln_g"], p["cf_ln_b"], p["w_cf_out"], p["w_o"])


def _ffn_call(layer, x, fst, p, *, G, R, final_norm):
    N, L, _ = x.shape
    NT = L // R
    M = G * R
    grid = (N // G, NT)
    row_spec = pl.BlockSpec((G, R, D_MODEL), lambda n, t: (n, t, 0))
    st_spec = pl.BlockSpec((G, FFN_K - 1, 2 * D_FF), lambda n, t: (n, 0, 0))
    in_specs = [
        row_spec, st_spec,
        _const_spec((1, D_MODEL), layer), _const_spec((D_MODEL, 2 * D_FF), layer),
        _const_spec((FFN_K, 2 * D_FF), layer), _const_spec((D_FF, D_MODEL), layer), _const_spec((1, D_MODEL), None),
    ]
    out_shape = [jax.ShapeDtypeStruct((N, L, D_MODEL), F32), jax.ShapeDtypeStruct((N, FFN_K - 1, 2 * D_FF), F32)]
    scratch = [
        pltpu.VMEM((M, D_MODEL), BF16),
        pltpu.VMEM((G, SC_HIST_PAD + R, 2 * D_FF), F32),
        pltpu.VMEM((M, D_FF), BF16),
    ]
    kern = functools.partial(_ffn_kernel, G=G, R=R, NT=NT, final_norm=final_norm)
    return pl.pallas_call(
        kern, grid=grid, in_specs=in_specs, out_specs=[row_spec, st_spec], out_shape=out_shape,
        scratch_shapes=scratch,
        compiler_params=pltpu.CompilerParams(dimension_semantics=("arbitrary", "arbitrary"),
                                             vmem_limit_bytes=VMEM_LIMIT_BYTES),
        name="ffn",
    )(x, fst, p["norm2_g"], p["w_up"], p["ffn_conv_w"], p["w_down"], p["final_g"])


def _rope_tables(pos):
    half = ROT_DIM // 2
    inv_freq = jnp.power(ROPE_THETA, -jnp.arange(0, ROT_DIM, 2, dtype=F32) / ROT_DIM)
    ang = pos.astype(F32)[:, None] * inv_freq[None, :]
    cos, sin = jnp.cos(ang), jnp.sin(ang)
    n = pos.shape[0]
    ones = jnp.ones((n, HEAD_DIM - ROT_DIM), F32)
    zeros = jnp.zeros((n, HEAD_DIM - ROT_DIM), F32)
    zh = jnp.zeros((n, half), F32)
    c = jnp.concatenate([cos, cos, ones], axis=1)
    sa = jnp.concatenate([zh, sin, zeros], axis=1)
    sb = jnp.concatenate([-sin, zh, zeros], axis=1)
    reps = LANES // HEAD_DIM
    return tuple(jnp.tile(a, (1, reps)) for a in (c, sa, sb))


def kernel(x_prompt, x_sample, cache_k, cache_v, state_sc, state_cf, state_ffn, meta_tokens, norm1_g, w_in, b_gate,
           attn_sinks, w_attn_out, sc_conv_w, w_sc_out, cf_conv_w, cf_ln_g, cf_ln_b, w_cf_out, w_o, norm2_g, w_up,
           ffn_conv_w, w_down, final_g):
    B, S, _ = x_prompt.shape
    DB, DS, _ = x_sample.shape
    depth = w_in.shape[0]
    lead = CHUNK - N_META
    LP = lead + N_META + S
    assert LP % PROMPT_TILE_ROWS == 0 and DS == CHUNK and DB % SAMPLE_SEQS_PER_TILE == 0
    assert cache_k.shape[2] == WINDOW

    p = dict(
        attn_sinks=attn_sinks,
        norm1_g=norm1_g[:, None, :], w_in=w_in.astype(BF16), b_gate=b_gate[:, None, :],
        w_attn_out=w_attn_out.astype(BF16), sc_conv_w=sc_conv_w, w_sc_out=w_sc_out.astype(BF16),
        cf_conv_w=cf_conv_w, cf_ln_g=cf_ln_g[:, None, :], cf_ln_b=cf_ln_b[:, None, :],
        w_cf_out=w_cf_out.astype(BF16), w_o=w_o.astype(BF16),
        norm2_g=norm2_g[:, None, :], w_up=w_up.astype(BF16), ffn_conv_w=ffn_conv_w, w_down=w_down.astype(BF16),
        final_g=final_g[None, :],
    )

    xp = jnp.concatenate([jnp.zeros((B, lead, D_MODEL), F32),
                          jnp.broadcast_to(meta_tokens[None], (B, N_META, D_MODEL)), x_prompt], axis=1)
    tabs_p = _rope_tables(jnp.arange(LP, dtype=jnp.int32) - lead)
    xs = x_sample
    tabs_s = _rope_tables(PAST_LEN + jnp.arange(DS, dtype=jnp.int32))

    zk = jnp.zeros((B, WINDOW, KV_W), F32)
    zsc = jnp.zeros((B, SC_K - 1, D_MODEL), F32)
    zcf = jnp.zeros((B, CF_K - 1, D_MODEL), F32)
    zffn = jnp.zeros((B, FFN_K - 1, 2 * D_FF), F32)
    ck = cache_k.reshape(depth, DB, WINDOW, KV_W)
    cv = cache_v.reshape(depth, DB, WINDOW, KV_W)

    st_p, st_s = [], []
    for i in range(depth):
        last = i == depth - 1
        xp, kp, vp, scp, cfp = _mixer_call(i, xp, tabs_p, zk, zk, zsc, zcf, p, G=1, R=PROMPT_TILE_ROWS, lead=lead)
        xp, fp = _ffn_call(i, xp, zffn, p, G=1, R=PROMPT_TILE_ROWS, final_norm=last)
        xs, ks, vs, scs, cfs = _mixer_call(i, xs, tabs_s, ck[i], cv[i], state_sc[i], state_cf[i], p,
                                           G=SAMPLE_SEQS_PER_TILE, R=CHUNK, lead=None)
        xs, fs = _ffn_call(i, xs, state_ffn[i], p, G=SAMPLE_SEQS_PER_TILE, R=CHUNK, final_norm=last)
        st_p.append((kp, vp, scp, cfp, fp))
        st_s.append((ks, vs, scs, cfs, fs))

    def stack(sts, j, n):
        a = jnp.stack([s[j] for s in sts])
        if j < 2:
            a = a.reshape(depth, n, WINDOW, N_KV_HEADS, HEAD_DIM)
        return a

    y_prompt = xp[:, lead + N_META:]
    return ((y_prompt, xs) + tuple(stack(st_p, j, B) for j in range(5)) + tuple(stack(st_s, j, DB) for j in range(5)))
```

```python
import functools

import jax
import jax.numpy as jnp
from jax import lax
from jax.experimental import pallas as pl
from jax.experimental.pallas import tpu as pltpu

D_MODEL = 1024
CHUNK = 64
N_META = 16
HEAD_DIM = 64
N_Q_HEADS = 16
N_KV_HEADS = 2
GROUP = N_Q_HEADS // N_KV_HEADS
ROT_DIM = HEAD_DIM // 4
ROPE_THETA = 500000.0
WINDOW = 128
ATTN_SCALE = HEAD_DIM ** -0.5
SC_K = 3
CF_K = 31
D_FF = 2816
FFN_K = 3
NORM_EPS = 1e-5
NEG_BIG = -1e30
PAST_LEN = 1024
Q_W = N_Q_HEADS * HEAD_DIM
KV_W = N_KV_HEADS * HEAD_DIM
N_IN = Q_W + 2 * KV_W + 3 * D_MODEL + 2 * D_MODEL + 3 * D_MODEL
OFF_Q = 0
OFF_SCB = Q_W + 2 * KV_W
OFF_SCC = OFF_SCB + D_MODEL
OFF_SCH = OFF_SCC + D_MODEL
OFF_CFA = OFF_SCH + D_MODEL
OFF_CFG = OFF_CFA + D_MODEL
OFF_GATE = OFF_CFG + D_MODEL

LANES = 128
SUBLANES = 8
VMEM_LIMIT_BYTES = 58 * 1024 * 1024

PROMPT_TILE_ROWS = 5 * CHUNK
SAMPLE_SEQS_PER_TILE = 4
COL_BLOCK = 512
FFN_COL_BLOCK = 256
CF_HIST_PAD = 32
CF_ROW_BLOCKS = (160, 64)
SC_HIST_PAD = SUBLANES

F32 = jnp.float32
BF16 = jnp.bfloat16


def _dot(a, b):
    return jnp.dot(a, b, preferred_element_type=F32)


def _rmsnorm_rows(x, g):
    ms = jnp.mean(x * x, axis=-1, keepdims=True)
    return (x * lax.rsqrt(ms + NORM_EPS)) * g


def _kv_variants(k):
    lane = lax.broadcasted_iota(jnp.int32, k.shape, 1)
    lo = lane < HEAD_DIM
    kr = pltpu.roll(k, HEAD_DIM, 1)
    z = jnp.zeros_like(k)
    out = (jnp.where(lo, k, z), jnp.where(lo, z, kr), jnp.where(lo, kr, z), jnp.where(lo, z, k))
    return tuple(o.astype(BF16) for o in out)


def _mixer_kernel(sinks_ref, x_ref, cos_ref, sa_ref, sb_ref, kst_ref, vst_ref, scst_ref, cfst_ref,
                  n1g_ref, win_ref, bg_ref, wao_ref, scw_ref, wso_ref, cfw_ref, lng_ref, lnb_ref, wco_ref, wo_ref,
                  xmid_ref, knew_ref, vnew_ref, scnew_ref, cfnew_ref,
                  xn_s, q_s, kb_s, vb_s, ubuf, gbuf, attn_s, scact_s, cfact_s, conv_s, merged_s, mb_s,
                  *, G, R, NT, lead, cf_rb):
    M = G * R
    nch = R // CHUNK
    t = pl.program_id(1)
    is_first = t == 0

    def load_state():
        for g in range(G):
            for j, kv in enumerate(_kv_variants(kst_ref[g])):
                kb_s[j, g, 0:WINDOW, :] = kv
            for j, vv in enumerate(_kv_variants(vst_ref[g])):
                vb_s[j, g, 0:WINDOW, :] = vv
            ubuf[g, SC_HIST_PAD - (SC_K - 1):SC_HIST_PAD, :] = scst_ref[g]
            gbuf[g, CF_HIST_PAD - (CF_K - 1):CF_HIST_PAD, :] = cfst_ref[g]

    def carry_state():
        for g in range(G):
            for j in range(4):
                kb_s[j, g, 0:WINDOW, :] = kb_s[j, g, R:R + WINDOW, :]
                vb_s[j, g, 0:WINDOW, :] = vb_s[j, g, R:R + WINDOW, :]
            ubuf[g, SC_HIST_PAD - (SC_K - 1):SC_HIST_PAD, :] = ubuf[g, R + SC_HIST_PAD - (SC_K - 1):R + SC_HIST_PAD, :]
            gbuf[g, CF_HIST_PAD - (CF_K - 1):CF_HIST_PAD, :] = gbuf[g, R + CF_HIST_PAD - (CF_K - 1):R + CF_HIST_PAD, :]

    if NT == 1:
        load_state()
    else:
        pl.when(is_first)(load_state)
        pl.when(jnp.logical_not(is_first))(carry_state)

    x = x_ref[...].reshape(M, D_MODEL)
    xn_s[...] = _rmsnorm_rows(x, n1g_ref[...]).astype(BF16)
    xnb = xn_s[...]

    def rows_tiled(ref):
        v = ref[...]
        return v if G == 1 else jnp.concatenate([v] * G, axis=0)

    cos_t, sa_t, sb_t = rows_tiled(cos_ref), rows_tiled(sa_ref), rows_tiled(sb_ref)

    def rotary(p):
        return p * cos_t + pltpu.roll(p, ROT_DIM // 2, 1) * sa_t + pltpu.roll(p, LANES - ROT_DIM // 2, 1) * sb_t

    k_rot = None
    v_new = None
    for j in range((Q_W + 2 * KV_W) // (2 * LANES)):
        pj = _dot(xnb, win_ref[:, 2 * LANES * j:2 * LANES * (j + 1)])
        for half in range(2):
            blk = 2 * j + half
            p = pj[:, LANES * half:LANES * (half + 1)]
            if blk < Q_W // LANES:
                q_s[:, LANES * blk:LANES * (blk + 1)] = (rotary(p) * ATTN_SCALE).astype(BF16)
            elif blk == Q_W // LANES:
                k_rot = rotary(p)
            else:
                v_new = p

    for g in range(G):
        kg = k_rot[g * R:(g + 1) * R]
        vg = v_new[g * R:(g + 1) * R]
        for j, kv in enumerate(_kv_variants(kg)):
            kb_s[j, g, WINDOW:WINDOW + R, :] = kv
        for j, vv in enumerate(_kv_variants(vg)):
            vb_s[j, g, WINDOW:WINDOW + R, :] = vv
        if R >= WINDOW:
            knew_ref[g] = kg[R - WINDOW:]
            vnew_ref[g] = vg[R - WINDOW:]
        else:
            knew_ref[g] = jnp.concatenate([kst_ref[g, R:WINDOW, :], kg], axis=0)
            vnew_ref[g] = jnp.concatenate([vst_ref[g, R:WINDOW, :], vg], axis=0)

    ncb = D_MODEL // COL_BLOCK
    for j in range(ncb):
        cs = slice(j * COL_BLOCK, (j + 1) * COL_BLOCK)
        glu = (_dot(xnb, win_ref[:, OFF_CFA + cs.start:OFF_CFA + cs.stop])
               * jax.nn.sigmoid(_dot(xnb, win_ref[:, OFF_CFG + cs.start:OFF_CFG + cs.stop])))
        for g in range(G):
            gbuf[g, CF_HIST_PAD:CF_HIST_PAD + R, cs] = glu[g * R:(g + 1) * R]
    for g in range(G):
        cfnew_ref[g] = gbuf[g, R + CF_HIST_PAD - (CF_K - 1):R + CF_HIST_PAD, :]

    band = 3 * CHUNK
    rblk = lax.broadcasted_iota(jnp.int32, (4 * CHUNK, 1), 0) // CHUNK
    sink_cols = []
    for h in range(N_KV_HEADS):
        for a in range(2):
            col = jnp.zeros((4 * CHUNK, 1), F32)
            for p in range(4):
                col = jnp.where(rblk == p, sinks_ref[GROUP * h + 2 * p + a], col)
            sink_cols.append(col)
    if lead is not None:
        first_invalid = jnp.where(is_first, WINDOW + lead, 0)
        kcol = lax.broadcasted_iota(jnp.int32, (1, band), 1)

    for g in range(G):
        for c in range(nch):
            row0 = g * R + c * CHUNK
            b0 = c * CHUNK
            for h in range(N_KV_HEADS):
                qst = jnp.concatenate(
                    [q_s[row0:row0 + CHUNK, (4 * h + p) * LANES:(4 * h + p + 1) * LANES] for p in range(4)], axis=0)
                o = jnp.zeros((4 * CHUNK, LANES), F32)
                for a in range(2):
                    kk = kb_s[2 * h + a, g, b0:b0 + band, :]
                    s = lax.dot_general(qst, kk, (((1,), (1,)), ((), ())), preferred_element_type=F32)
                    if lead is not None and b0 < WINDOW + lead:
                        s = jnp.where(kcol + b0 >= first_invalid, s, NEG_BIG)
                    sink = sink_cols[2 * h + a]
                    m = jnp.maximum(jnp.max(s, axis=-1, keepdims=True), sink)
                    e = jnp.exp(s - m)
                    den = jnp.sum(e, axis=-1, keepdims=True) + jnp.exp(sink - m)
                    vv = vb_s[2 * h + a, g, b0:b0 + band, :]
                    o = o + _dot(e.astype(BF16), vv) * (1.0 / den)
                for p in range(4):
                    attn_s[row0:row0 + CHUNK, (4 * h + p) * LANES:(4 * h + p + 1) * LANES] = (
                        o[p * CHUNK:(p + 1) * CHUNK].astype(BF16))

    for j in range(ncb):
        cs = slice(j * COL_BLOCK, (j + 1) * COL_BLOCK)
        sc_b = _dot(xnb, win_ref[:, OFF_SCB + cs.start:OFF_SCB + cs.stop])
        u = (_dot(xnb, win_ref[:, OFF_SCC + cs.start:OFF_SCC + cs.stop])
             * _dot(xnb, win_ref[:, OFF_SCH + cs.start:OFF_SCH + cs.stop]))
        for g in range(G):
            ubuf[g, SC_HIST_PAD:SC_HIST_PAD + R, cs] = u[g * R:(g + 1) * R]
        conv = []
        for g in range(G):
            acc = scw_ref[SC_K - 1:SC_K, cs] * u[g * R:(g + 1) * R]
            for k in range(SC_K - 1):
                o0 = SC_HIST_PAD - (SC_K - 1) + k
                acc = acc + scw_ref[k:k + 1, cs] * ubuf[g, o0:o0 + R, cs]
            conv.append(acc)
        conv = conv[0] if G == 1 else jnp.concatenate(conv, axis=0)
        scact_s[:, cs] = (sc_b * conv).astype(BF16)
    for g in range(G):
        scnew_ref[g] = ubuf[g, R + SC_HIST_PAD - (SC_K - 1):R + SC_HIST_PAD, :]

    def gate(branch, cs):
        pre = _dot(xnb, win_ref[:, OFF_GATE + branch * D_MODEL + cs.start:OFF_GATE + branch * D_MODEL + cs.stop])
        return jax.nn.sigmoid(pre + bg_ref[:, branch * D_MODEL + cs.start:branch * D_MODEL + cs.stop])

    for j in range(ncb):
        cs = slice(j * COL_BLOCK, (j + 1) * COL_BLOCK)
        merged_s[:, cs] = (gate(0, cs) * _dot(attn_s[...], wao_ref[:, cs])
                           + gate(1, cs) * _dot(scact_s[...], wso_ref[:, cs]))

    hist0 = CF_HIST_PAD - (CF_K - 1)
    for cb in range(D_MODEL // LANES):
        lanes = slice(cb * LANES, (cb + 1) * LANES)
        for g in range(G):
            for rb in range(R // cf_rb):
                t0 = rb * cf_rb
                out = None
                for s in range(SUBLANES):
                    nrows = cf_rb if s == 0 else cf_rb + SUBLANES
                    acc = None
                    for j in range((hist0 + CF_K - 1) // SUBLANES + 1):
                        k = SUBLANES * j + s - hist0
                        if 0 <= k < CF_K:
                            term = cfw_ref[k:k + 1, lanes] * gbuf[g, t0 + SUBLANES * j:t0 + SUBLANES * j + nrows, lanes]
                            acc = term if acc is None else acc + term
                    sh = acc if s == 0 else acc[s:s + cf_rb]
                    out = sh if out is None else out + sh
                conv_s[g * R + t0:g * R + t0 + cf_rb, lanes] = out

    cv = conv_s[...]
    mu = jnp.mean(cv, axis=-1, keepdims=True)
    xc = cv - mu
    var = jnp.mean(xc * xc, axis=-1, keepdims=True)
    y = xc * lax.rsqrt(var + NORM_EPS) * lng_ref[...] + lnb_ref[...]
    cfact_s[...] = (y * jax.nn.sigmoid(y)).astype(BF16)
    for j in range(ncb):
        cs = slice(j * COL_BLOCK, (j + 1) * COL_BLOCK)
        mb_s[:, cs] = (merged_s[:, cs] + gate(2, cs) * _dot(cfact_s[...], wco_ref[:, cs])).astype(BF16)

    out = x + _dot(mb_s[...], wo_ref[...])
    if lead is not None:
        row = lax.broadcasted_iota(jnp.int32, (M, 1), 0)
        out = jnp.where(jnp.logical_or(jnp.logical_not(is_first), row >= lead), out, 0.0)
    xmid_ref[...] = out.reshape(G, R, D_MODEL)


def _ffn_kernel(x_ref, fst_ref, n2g_ref, wup_ref, fcw_ref, wdn_ref, fing_ref,
                xout_ref, fnew_ref,
                xn_s, hbuf, act_s,
                *, G, R, NT, final_norm):
    M = G * R
    t = pl.program_id(1)
    is_first = t == 0
    h0 = SC_HIST_PAD - (FFN_K - 1)

    def load_state():
        for g in range(G):
            hbuf[g, h0:SC_HIST_PAD, :] = fst_ref[g]

    def carry_state():
        for g in range(G):
            hbuf[g, h0:SC_HIST_PAD, :] = hbuf[g, R + h0:R + SC_HIST_PAD, :]

    if NT == 1:
        load_state()
    else:
        pl.when(is_first)(load_state)
        pl.when(jnp.logical_not(is_first))(carry_state)

    x = x_ref[...].reshape(M, D_MODEL)
    xn_s[...] = _rmsnorm_rows(x, n2g_ref[...]).astype(BF16)
    xnb = xn_s[...]

    def conv3(cs):
        h = _dot(xnb, wup_ref[:, cs])
        out = []
        for g in range(G):
            hg = h[g * R:(g + 1) * R]
            hbuf[g, SC_HIST_PAD:SC_HIST_PAD + R, cs] = hg
            acc = fcw_ref[FFN_K - 1:FFN_K, cs] * hg
            for k in range(FFN_K - 1):
                acc = acc + fcw_ref[k:k + 1, cs] * hbuf[g, h0 + k:h0 + k + R, cs]
            out.append(acc)
        return out[0] if G == 1 else jnp.concatenate(out, axis=0)

    for j in range(D_FF // FFN_COL_BLOCK):
        cg = slice(j * FFN_COL_BLOCK, (j + 1) * FFN_COL_BLOCK)
        cu = slice(D_FF + j * FFN_COL_BLOCK, D_FF + (j + 1) * FFN_COL_BLOCK)
        hg = conv3(cg)
        hu = conv3(cu)
        act_s[:, cg] = (hg * jax.nn.sigmoid(hg) * hu).astype(BF16)
    for g in range(G):
        fnew_ref[g] = hbuf[g, R + h0:R + SC_HIST_PAD, :]

    out = x + _dot(act_s[...], wdn_ref[...])
    if final_norm:
        out = _rmsnorm_rows(out, fing_ref[...])
    xout_ref[...] = out.reshape(G, R, D_MODEL)


def _const_spec(shape, layer):
    nd = len(shape)
    if layer is None:
        return pl.BlockSpec(shape, lambda n, t: (0,) * nd, pipeline_mode=pl.Buffered(1))
    return pl.BlockSpec((None,) + shape, lambda n, t: (layer,) + (0,) * nd, pipeline_mode=pl.Buffered(1))


def _mixer_call(layer, x, tabs, kst, vst, scst, cfst, p, *, G, R, lead):
    N, L, _ = x.shape
    NT = L // R
    M = G * R
    grid = (N // G, NT)
    row_spec = pl.BlockSpec((G, R, D_MODEL), lambda n, t: (n, t, 0))
    tab_spec = pl.BlockSpec((R, LANES), lambda n, t: (t, 0))

    def st_spec(rows, width):
        return pl.BlockSpec((G, rows, width), lambda n, t: (n, 0, 0))

    in_specs = [
        pl.BlockSpec(memory_space=pltpu.SMEM),
        row_spec, tab_spec, tab_spec, tab_spec,
        st_spec(WINDOW, KV_W), st_spec(WINDOW, KV_W), st_spec(SC_K - 1, D_MODEL), st_spec(CF_K - 1, D_MODEL),
        _const_spec((1, D_MODEL), layer), _const_spec((D_MODEL, N_IN), layer), _const_spec((1, 3 * D_MODEL), layer),
        _const_spec((Q_W, D_MODEL), layer), _const_spec((SC_K, D_MODEL), layer), _const_spec((D_MODEL, D_MODEL), layer),
        _const_spec((CF_K, D_MODEL), layer), _const_spec((1, D_MODEL), layer), _const_spec((1, D_MODEL), layer),
        _const_spec((D_MODEL, D_MODEL), layer), _const_spec((D_MODEL, D_MODEL), layer),
    ]
    out_specs = [row_spec, st_spec(WINDOW, KV_W), st_spec(WINDOW, KV_W), st_spec(SC_K - 1, D_MODEL),
                 st_spec(CF_K - 1, D_MODEL)]
    out_shape = [
        jax.ShapeDtypeStruct((N, L, D_MODEL), F32),
        jax.ShapeDtypeStruct((N, WINDOW, KV_W), F32), jax.ShapeDtypeStruct((N, WINDOW, KV_W), F32),
        jax.ShapeDtypeStruct((N, SC_K - 1, D_MODEL), F32), jax.ShapeDtypeStruct((N, CF_K - 1, D_MODEL), F32),
    ]
    scratch = [
        pltpu.VMEM((M, D_MODEL), BF16),
        pltpu.VMEM((M, Q_W), BF16),
        pltpu.VMEM((4, G, WINDOW + R, LANES), BF16),
        pltpu.VMEM((4, G, WINDOW + R, LANES), BF16),
        pltpu.VMEM((G, SC_HIST_PAD + R, D_MODEL), F32),
        pltpu.VMEM((G, CF_HIST_PAD + R, D_MODEL), F32),
        pltpu.VMEM((M, D_MODEL), BF16),
        pltpu.VMEM((M, D_MODEL), BF16),
        pltpu.VMEM((M, D_MODEL), BF16),
        pltpu.VMEM((M, D_MODEL), F32),
        pltpu.VMEM((M, D_MODEL), F32),
        pltpu.VMEM((M, D_MODEL), BF16),
    ]
    cf_rb = max(rb for rb in CF_ROW_BLOCKS if R % rb == 0)
    kern = functools.partial(_mixer_kernel, G=G, R=R, NT=NT, lead=lead, cf_rb=cf_rb)
    return pl.pallas_call(
        kern, grid=grid, in_specs=in_specs, out_specs=out_specs, out_shape=out_shape, scratch_shapes=scratch,
        compiler_params=pltpu.CompilerParams(dimension_semantics=("arbitrary", "arbitrary"),
                                             vmem_limit_bytes=VMEM_LIMIT_BYTES),
        name="mixer",
    )(p["attn_sinks"][layer], x, *tabs, kst, vst, scst, cfst,
      p["norm1_g"], p["w_in"], p["b_gate"], p["w_attn_out"], p["sc_conv_w"], p["w_sc_out"],
      p["cf_conv_w"], p["cf_ln_g"], p["cf_ln_b"], p["w_cf_out"], p["w_o"])


def _ffn_call(layer, x, fst, p, *, G, R, final_norm):
    N, L, _ = x.shape
    NT = L // R
    M = G * R
    grid = (N // G, NT)
    row_spec = pl.BlockSpec((G, R, D_MODEL), lambda n, t: (n, t, 0))
    st_spec = pl.BlockSpec((G, FFN_K - 1, 2 * D_FF), lambda n, t: (n, 0, 0))
    in_specs = [
        row_spec, st_spec,
        _const_spec((1, D_MODEL), layer), _const_spec((D_MODEL, 2 * D_FF), layer),
        _const_spec((FFN_K, 2 * D_FF), layer), _const_spec((D_FF, D_MODEL), layer), _const_spec((1, D_MODEL), None),
    ]
    out_shape = [jax.ShapeDtypeStruct((N, L, D_MODEL), F32), jax.ShapeDtypeStruct((N, FFN_K - 1, 2 * D_FF), F32)]
    scratch = [
        pltpu.VMEM((M, D_MODEL), BF16),
        pltpu.VMEM((G, SC_HIST_PAD + R, 2 * D_FF), F32),
        pltpu.VMEM((M, D_FF), BF16),
    ]
    kern = functools.partial(_ffn_kernel, G=G, R=R, NT=NT, final_norm=final_norm)
    return pl.pallas_call(
        kern, grid=grid, in_specs=in_specs, out_specs=[row_spec, st_spec], out_shape=out_shape,
        scratch_shapes=scratch,
        compiler_params=pltpu.CompilerParams(dimension_semantics=("arbitrary", "arbitrary"),
                                             vmem_limit_bytes=VMEM_LIMIT_BYTES),
        name="ffn",
    )(x, fst, p["norm2_g"], p["w_up"], p["ffn_conv_w"], p["w_down"], p["final_g"])


def _rope_tables(pos):
    half = ROT_DIM // 2
    inv_freq = jnp.power(ROPE_THETA, -jnp.arange(0, ROT_DIM, 2, dtype=F32) / ROT_DIM)
    ang = pos.astype(F32)[:, None] * inv_freq[None, :]
    cos, sin = jnp.cos(ang), jnp.sin(ang)
    n = pos.shape[0]
    ones = jnp.ones((n, HEAD_DIM - ROT_DIM), F32)
    zeros = jnp.zeros((n, HEAD_DIM - ROT_DIM), F32)
    zh = jnp.zeros((n, half), F32)
    c = jnp.concatenate([cos, cos, ones], axis=1)
    sa = jnp.concatenate([zh, sin, zeros], axis=1)
    sb = jnp.concatenate([-sin, zh, zeros], axis=1)
    reps = LANES // HEAD_DIM
    return tuple(jnp.tile(a, (1, reps)) for a in (c, sa, sb))


def kernel(x_prompt, x_sample, cache_k, cache_v, state_sc, state_cf, state_ffn, meta_tokens, norm1_g, w_in, b_gate,
           attn_sinks, w_attn_out, sc_conv_w, w_sc_out, cf_conv_w, cf_ln_g, cf_ln_b, w_cf_out, w_o, norm2_g, w_up,
           ffn_conv_w, w_down, final_g):
    B, S, _ = x_prompt.shape
    DB, DS, _ = x_sample.shape
    depth = w_in.shape[0]
    lead = CHUNK - N_META
    LP = lead + N_META + S
    assert LP % PROMPT_TILE_ROWS == 0 and DS == CHUNK and DB % SAMPLE_SEQS_PER_TILE == 0
    assert cache_k.shape[2] == WINDOW

    p = dict(
        attn_sinks=attn_sinks,
        norm1_g=norm1_g[:, None, :], w_in=w_in.astype(BF16), b_gate=b_gate[:, None, :],
        w_attn_out=w_attn_out.astype(BF16), sc_conv_w=sc_conv_w, w_sc_out=w_sc_out.astype(BF16),
        cf_conv_w=cf_conv_w, cf_ln_g=cf_ln_g[:, None, :], cf_ln_b=cf_ln_b[:, None, :],
        w_cf_out=w_cf_out.astype(BF16), w_o=w_o.astype(BF16),
        norm2_g=norm2_g[:, None, :], w_up=w_up.astype(BF16), ffn_conv_w=ffn_conv_w, w_down=w_down.astype(BF16),
        final_g=final_g[None, :],
    )

    xp = jnp.concatenate([jnp.zeros((B, lead, D_MODEL), F32),
                          jnp.broadcast_to(meta_tokens[None], (B, N_META, D_MODEL)), x_prompt], axis=1)
    tabs_p = _rope_tables(jnp.arange(LP, dtype=jnp.int32) - lead)
    xs = x_sample
    tabs_s = _rope_tables(PAST_LEN + jnp.arange(DS, dtype=jnp.int32))

    zk = jnp.zeros((B, WINDOW, KV_W), F32)
    zsc = jnp.zeros((B, SC_K - 1, D_MODEL), F32)
    zcf = jnp.zeros((B, CF_K - 1, D_MODEL), F32)
    zffn = jnp.zeros((B, FFN_K - 1, 2 * D_FF), F32)
    ck = cache_k.reshape(depth, DB, WINDOW, KV_W)
    cv = cache_v.reshape(depth, DB, WINDOW, KV_W)

    st_p, st_s = [], []
    for i in range(depth):
        last = i == depth - 1
        xp, kp, vp, scp, cfp = _mixer_call(i, xp, tabs_p, zk, zk, zsc, zcf, p, G=1, R=PROMPT_TILE_ROWS, lead=lead)
        xp, fp = _ffn_call(i, xp, zffn, p, G=1, R=PROMPT_TILE_ROWS, final_norm=last)
        xs, ks, vs, scs, cfs = _mixer_call(i, xs, tabs_s, ck[i], cv[i], state_sc[i], state_cf[i], p,
                                           G=SAMPLE_SEQS_PER_TILE, R=CHUNK, lead=None)
        xs, fs = _ffn_call(i, xs, state_ffn[i], p, G=SAMPLE_SEQS_PER_TILE, R=CHUNK, final_norm=last)
        st_p.append((kp, vp, scp, cfp, fp))
        st_s.append((ks, vs, scs, cfs, fs))

    def stack(sts, j, n):
        a = jnp.stack([s[j] for s in sts])
        if j < 2:
            a = a.reshape(depth, n, WINDOW, N_KV_HEADS, HEAD_DIM)
        return a

    y_prompt = xp[:, lead + N_META:]
    return ((y_prompt, xs) + tuple(stack(st_p, j, B) for j in range(5)) + tuple(stack(st_s, j, DB) for j in range(5)))
```

```python
import functools

import jax
import jax.numpy as jnp
from jax import lax
from jax.experimental import pallas as pl
from jax.experimental.pallas import tpu as pltpu

D_MODEL = 1024
CHUNK = 64
N_META = 16
HEAD_DIM = 64
N_Q_HEADS = 16
N_KV_HEADS = 2
GROUP = N_Q_HEADS // N_KV_HEADS
ROT_DIM = HEAD_DIM // 4
ROPE_THETA = 500000.0
WINDOW = 128
ATTN_SCALE = HEAD_DIM ** -0.5
SC_K = 3
CF_K = 31
D_FF = 2816
FFN_K = 3
NORM_EPS = 1e-5
NEG_BIG = -1e30
PAST_LEN = 1024
Q_W = N_Q_HEADS * HEAD_DIM
KV_W = N_KV_HEADS * HEAD_DIM
N_IN = Q_W + 2 * KV_W + 3 * D_MODEL + 2 * D_MODEL + 3 * D_MODEL
OFF_Q = 0
OFF_SCB = Q_W + 2 * KV_W
OFF_SCC = OFF_SCB + D_MODEL
OFF_SCH = OFF_SCC + D_MODEL
OFF_CFA = OFF_SCH + D_MODEL
OFF_CFG = OFF_CFA + D_MODEL
OFF_GATE = OFF_CFG + D_MODEL

LANES = 128
SUBLANES = 8
VMEM_LIMIT_BYTES = 58 * 1024 * 1024

PROMPT_TILE_ROWS = 5 * CHUNK
SAMPLE_SEQS_PER_TILE = 4
COL_BLOCK = 512
FFN_COL_BLOCK = 256
CF_HIST_PAD = 32
CF_ROW_BLOCKS = (64,)
SC_HIST_PAD = SUBLANES

F32 = jnp.float32
BF16 = jnp.bfloat16


def _dot(a, b):
    return jnp.dot(a, b, preferred_element_type=F32)


def _rmsnorm_rows(x, g):
    ms = jnp.mean(x * x, axis=-1, keepdims=True)
    return (x * lax.rsqrt(ms + NORM_EPS)) * g


def _interleave(primary, secondary):
    done = 0
    for i, f in enumerate(primary):
        f()
        want = (i + 1) * len(secondary) // len(primary)
        while done < want:
            secondary[done]()
            done += 1


def _kv_variants(k):
    lane = lax.broadcasted_iota(jnp.int32, k.shape, 1)
    lo = lane < HEAD_DIM
    kr = pltpu.roll(k, HEAD_DIM, 1)
    z = jnp.zeros_like(k)
    out = (jnp.where(lo, k, z), jnp.where(lo, z, kr), jnp.where(lo, kr, z), jnp.where(lo, z, k))
    return tuple(o.astype(BF16) for o in out)


def _mixer_kernel(sinks_ref, x_ref, cos_ref, sa_ref, sb_ref, kst_ref, vst_ref, scst_ref, cfst_ref,
                  n1g_ref, win_ref, bg_ref, wao_ref, scw_ref, wso_ref, cfw_ref, lng_ref, lnb_ref, wco_ref, wo_ref,
                  xmid_ref, knew_ref, vnew_ref, scnew_ref, cfnew_ref,
                  xn_s, q_s, kb_s, vb_s, ubuf, gbuf, attn_s, scact_s, cfact_s, conv_s, merged_s, gate_s, mb_s,
                  *, G, R, NT, lead, cf_rb):
    M = G * R
    nch = R // CHUNK
    t = pl.program_id(1)
    is_first = t == 0

    def load_state():
        for g in range(G):
            for j, kv in enumerate(_kv_variants(kst_ref[g])):
                kb_s[j, g, 0:WINDOW, :] = kv
            for j, vv in enumerate(_kv_variants(vst_ref[g])):
                vb_s[j, g, 0:WINDOW, :] = vv
            ubuf[g, SC_HIST_PAD - (SC_K - 1):SC_HIST_PAD, :] = scst_ref[g]
            gbuf[g, CF_HIST_PAD - (CF_K - 1):CF_HIST_PAD, :] = cfst_ref[g]

    def carry_state():
        for g in range(G):
            for j in range(4):
                kb_s[j, g, 0:WINDOW, :] = kb_s[j, g, R:R + WINDOW, :]
                vb_s[j, g, 0:WINDOW, :] = vb_s[j, g, R:R + WINDOW, :]
            ubuf[g, SC_HIST_PAD - (SC_K - 1):SC_HIST_PAD, :] = ubuf[g, R + SC_HIST_PAD - (SC_K - 1):R + SC_HIST_PAD, :]
            gbuf[g, CF_HIST_PAD - (CF_K - 1):CF_HIST_PAD, :] = gbuf[g, R + CF_HIST_PAD - (CF_K - 1):R + CF_HIST_PAD, :]

    if NT == 1:
        load_state()
    else:
        pl.when(is_first)(load_state)
        pl.when(jnp.logical_not(is_first))(carry_state)

    xn_s[...] = _rmsnorm_rows(x_ref[...].reshape(M, D_MODEL), n1g_ref[...]).astype(BF16)

    def rotary(p):
        def rows_tiled(ref):
            v = ref[...]
            return v if G == 1 else jnp.concatenate([v] * G, axis=0)

        return (p * rows_tiled(cos_ref) + pltpu.roll(p, ROT_DIM // 2, 1) * rows_tiled(sa_ref)
                + pltpu.roll(p, LANES - ROT_DIM // 2, 1) * rows_tiled(sb_ref))

    def new_kv_rows(kv, buf, st_ref, new_ref):
        for g in range(G):
            rows = kv[g * R:(g + 1) * R]
            for j, var in enumerate(_kv_variants(rows)):
                buf[j, g, WINDOW:WINDOW + R, :] = var
            if R >= WINDOW:
                new_ref[g] = rows[R - WINDOW:]
            else:
                new_ref[g] = jnp.concatenate([st_ref[g, R:WINDOW, :], rows], axis=0)

    ncb = D_MODEL // COL_BLOCK
    xnb = xn_s[...]
    for j in range(ncb):
        cs = slice(j * COL_BLOCK, (j + 1) * COL_BLOCK)
        glu = (_dot(xnb, win_ref[:, OFF_CFA + cs.start:OFF_CFA + cs.stop])
               * jax.nn.sigmoid(_dot(xnb, win_ref[:, OFF_CFG + cs.start:OFF_CFG + cs.stop])))
        for g in range(G):
            gbuf[g, CF_HIST_PAD:CF_HIST_PAD + R, cs] = glu[g * R:(g + 1) * R]
    for g in range(G):
        cfnew_ref[g] = gbuf[g, R + CF_HIST_PAD - (CF_K - 1):R + CF_HIST_PAD, :]
    pkv = _dot(xnb, win_ref[:, Q_W:Q_W + 2 * KV_W])
    new_kv_rows(rotary(pkv[:, :KV_W]), kb_s, kst_ref, knew_ref)
    new_kv_rows(pkv[:, KV_W:], vb_s, vst_ref, vnew_ref)

    band = 3 * CHUNK

    def sink_column(h, a):
        rblk = lax.broadcasted_iota(jnp.int32, (4 * CHUNK, 1), 0) // CHUNK
        col = jnp.zeros((4 * CHUNK, 1), F32)
        for p in range(4):
            col = jnp.where(rblk == p, sinks_ref[GROUP * h + 2 * p + a], col)
        return col

    def attn_unit(g, c, h):
        row0 = g * R + c * CHUNK
        b0 = c * CHUNK
        if lead is not None:
            first_invalid = jnp.where(is_first, WINDOW + lead, 0)
            kcol = lax.broadcasted_iota(jnp.int32, (1, band), 1)
        qst = jnp.concatenate(
            [q_s[row0:row0 + CHUNK, (4 * h + p) * LANES:(4 * h + p + 1) * LANES] for p in range(4)], axis=0)
        o = jnp.zeros((4 * CHUNK, LANES), F32)
        for a in range(2):
            kk = kb_s[2 * h + a, g, b0:b0 + band, :]
            s = lax.dot_general(qst, kk, (((1,), (1,)), ((), ())), preferred_element_type=F32)
            if lead is not None and b0 < WINDOW + lead:
                s = jnp.where(kcol + b0 >= first_invalid, s, NEG_BIG)
            sink = sink_column(h, a)
            m = jnp.maximum(jnp.max(s, axis=-1, keepdims=True), sink)
            e = jnp.exp(s - m)
            den = jnp.sum(e, axis=-1, keepdims=True) + jnp.exp(sink - m)
            vv = vb_s[2 * h + a, g, b0:b0 + band, :]
            o = o + _dot(e.astype(BF16), vv) * (1.0 / den)
        for p in range(4):
            attn_s[row0:row0 + CHUNK, (4 * h + p) * LANES:(4 * h + p + 1) * LANES] = (
                o[p * CHUNK:(p + 1) * CHUNK].astype(BF16))

    def cols(c0, width=COL_BLOCK):
        if isinstance(c0, int):
            return slice(c0, c0 + width)
        return pl.ds(pl.multiple_of(c0, LANES), width)

    def q_unit(xnb, c0):
        pj = _dot(xnb, win_ref[:, cols(OFF_Q + c0)])
        for half in range(COL_BLOCK // LANES):
            p = pj[:, LANES * half:LANES * (half + 1)]
            q_s[:, cols(c0 + LANES * half, LANES)] = (rotary(p) * ATTN_SCALE).astype(BF16)

    def sc_unit(xnb, c0):
        cs = cols(c0)
        sc_b = _dot(xnb, win_ref[:, cols(OFF_SCB + c0)])
        u = _dot(xnb, win_ref[:, cols(OFF_SCC + c0)]) * _dot(xnb, win_ref[:, cols(OFF_SCH + c0)])
        for g in range(G):
            ubuf[g, SC_HIST_PAD:SC_HIST_PAD + R, cs] = u[g * R:(g + 1) * R]
        conv = []
        for g in range(G):
            acc = scw_ref[SC_K - 1:SC_K, cs] * u[g * R:(g + 1) * R]
            for k in range(SC_K - 1):
                o0 = SC_HIST_PAD - (SC_K - 1) + k
                acc = acc + scw_ref[k:k + 1, cs] * ubuf[g, o0:o0 + R, cs]
            conv.append(acc)
        conv = conv[0] if G == 1 else jnp.concatenate(conv, axis=0)
        scact_s[:, cs] = (sc_b * conv).astype(BF16)

    def gate_unit(xnb, branch, c0):
        pre = _dot(xnb, win_ref[:, cols(OFF_GATE + branch * D_MODEL + c0)])
        gate_s[branch, :, cols(c0)] = jax.nn.sigmoid(pre + bg_ref[:, cols(branch * D_MODEL + c0)])

    def sc_merge_unit(j):
        cs = slice(j * COL_BLOCK, (j + 1) * COL_BLOCK)
        merged_s[:, cs] = gate_s[1, :, cs] * _dot(scact_s[...], wso_ref[:, cs])

    hist0 = CF_HIST_PAD - (CF_K - 1)

    def conv31_unit(c0, g, rb):
        lanes = cols(c0, LANES)
        t0 = rb * cf_rb
        out = None
        for s in range(SUBLANES):
            nrows = cf_rb if s == 0 else cf_rb + SUBLANES
            acc = None
            for j in range((hist0 + CF_K - 1) // SUBLANES + 1):
                k = SUBLANES * j + s - hist0
                if 0 <= k < CF_K:
                    term = cfw_ref[k:k + 1, lanes] * gbuf[g, t0 + SUBLANES * j:t0 + SUBLANES * j + nrows, lanes]
                    acc = term if acc is None else acc + term
            sh = acc if s == 0 else acc[s:s + cf_rb]
            out = sh if out is None else out + sh
        conv_s[g * R + t0:g * R + t0 + cf_rb, lanes] = out

    for j in range(ncb):
        c0 = j * COL_BLOCK
        for half in range(COL_BLOCK // LANES):
            for g in range(G):
                for rb in range(R // cf_rb):
                    conv31_unit(c0 + LANES * half, g, rb)
        xnb = xn_s[...]
        q_unit(xnb, c0)
        sc_unit(xnb, c0)
        for branch in range(3):
            gate_unit(xnb, branch, c0)
    for g in range(G):
        scnew_ref[g] = ubuf[g, R + SC_HIST_PAD - (SC_K - 1):R + SC_HIST_PAD, :]

    attn_units = [functools.partial(attn_unit, g, c, h)
                  for g in range(G) for c in range(nch) for h in range(N_KV_HEADS)]
    _interleave(attn_units, [functools.partial(sc_merge_unit, j) for j in range(ncb)])

    cv = conv_s[...]
    mu = jnp.mean(cv, axis=-1, keepdims=True)
    xc = cv - mu
    var = jnp.mean(xc * xc, axis=-1, keepdims=True)
    y = xc * lax.rsqrt(var + NORM_EPS) * lng_ref[...] + lnb_ref[...]
    cfact_s[...] = (y * jax.nn.sigmoid(y)).astype(BF16)
    for j in range(ncb):
        cs = slice(j * COL_BLOCK, (j + 1) * COL_BLOCK)
        mb_s[:, cs] = (merged_s[:, cs] + gate_s[0, :, cs] * _dot(attn_s[...], wao_ref[:, cs])
                       + gate_s[2, :, cs] * _dot(cfact_s[...], wco_ref[:, cs])).astype(BF16)

    out = x_ref[...].reshape(M, D_MODEL) + _dot(mb_s[...], wo_ref[...])
    if lead is not None:
        row = lax.broadcasted_iota(jnp.int32, (M, 1), 0)
        out = jnp.where(jnp.logical_or(jnp.logical_not(is_first), row >= lead), out, 0.0)
    xmid_ref[...] = out.reshape(G, R, D_MODEL)


def _ffn_kernel(x_ref, fst_ref, n2g_ref, wup_ref, fcw_ref, wdn_ref, fing_ref,
                xout_ref, fnew_ref,
                xn_s, hbuf, act_s,
                *, G, R, NT, final_norm):
    M = G * R
    t = pl.program_id(1)
    is_first = t == 0
    h0 = SC_HIST_PAD - (FFN_K - 1)

    def load_state():
        for g in range(G):
            hbuf[g, h0:SC_HIST_PAD, :] = fst_ref[g]

    def carry_state():
        for g in range(G):
            hbuf[g, h0:SC_HIST_PAD, :] = hbuf[g, R + h0:R + SC_HIST_PAD, :]

    if NT == 1:
        load_state()
    else:
        pl.when(is_first)(load_state)
        pl.when(jnp.logical_not(is_first))(carry_state)

    x = x_ref[...].reshape(M, D_MODEL)
    xn_s[...] = _rmsnorm_rows(x, n2g_ref[...]).astype(BF16)
    xnb = xn_s[...]

    def conv3(cs):
        h = _dot(xnb, wup_ref[:, cs])
        out = []
        for g in range(G):
            hg = h[g * R:(g + 1) * R]
            hbuf[g, SC_HIST_PAD:SC_HIST_PAD + R, cs] = hg
            acc = fcw_ref[FFN_K - 1:FFN_K, cs] * hg
            for k in range(FFN_K - 1):
                acc = acc + fcw_ref[k:k + 1, cs] * hbuf[g, h0 + k:h0 + k + R, cs]
            out.append(acc)
        return out[0] if G == 1 else jnp.concatenate(out, axis=0)

    for j in range(D_FF // FFN_COL_BLOCK):
        cg = slice(j * FFN_COL_BLOCK, (j + 1) * FFN_COL_BLOCK)
        cu = slice(D_FF + j * FFN_COL_BLOCK, D_FF + (j + 1) * FFN_COL_BLOCK)
        hg = conv3(cg)
        hu = conv3(cu)
        act_s[:, cg] = (hg * jax.nn.sigmoid(hg) * hu).astype(BF16)
    for g in range(G):
        fnew_ref[g] = hbuf[g, R + h0:R + SC_HIST_PAD, :]

    out = x + _dot(act_s[...], wdn_ref[...])
    if final_norm:
        out = _rmsnorm_rows(out, fing_ref[...])
    xout_ref[...] = out.reshape(G, R, D_MODEL)


def _const_spec(shape, layer):
    nd = len(shape)
    if layer is None:
        return pl.BlockSpec(shape, lambda n, t: (0,) * nd, pipeline_mode=pl.Buffered(1))
    return pl.BlockSpec((None,) + shape, lambda n, t: (layer,) + (0,) * nd, pipeline_mode=pl.Buffered(1))


def _mixer_call(layer, x, tabs, kst, vst, scst, cfst, p, *, G, R, lead):
    N, L, _ = x.shape
    NT = L // R
    M = G * R
    grid = (N // G, NT)
    row_spec = pl.BlockSpec((G, R, D_MODEL), lambda n, t: (n, t, 0))
    tab_spec = pl.BlockSpec((R, LANES), lambda n, t: (t, 0))

    def st_spec(rows, width):
        return pl.BlockSpec((G, rows, width), lambda n, t: (n, 0, 0))

    in_specs = [
        pl.BlockSpec(memory_space=pltpu.SMEM),
        row_spec, tab_spec, tab_spec, tab_spec,
        st_spec(WINDOW, KV_W), st_spec(WINDOW, KV_W), st_spec(SC_K - 1, D_MODEL), st_spec(CF_K - 1, D_MODEL),
        _const_spec((1, D_MODEL), layer), _const_spec((D_MODEL, N_IN), layer), _const_spec((1, 3 * D_MODEL), layer),
        _const_spec((Q_W, D_MODEL), layer), _const_spec((SC_K, D_MODEL), layer), _const_spec((D_MODEL, D_MODEL), layer),
        _const_spec((CF_K, D_MODEL), layer), _const_spec((1, D_MODEL), layer), _const_spec((1, D_MODEL), layer),
        _const_spec((D_MODEL, D_MODEL), layer), _const_spec((D_MODEL, D_MODEL), layer),
    ]
    out_specs = [row_spec, st_spec(WINDOW, KV_W), st_spec(WINDOW, KV_W), st_spec(SC_K - 1, D_MODEL),
                 st_spec(CF_K - 1, D_MODEL)]
    out_shape = [
        jax.ShapeDtypeStruct((N, L, D_MODEL), F32),
        jax.ShapeDtypeStruct((N, WINDOW, KV_W), F32), jax.ShapeDtypeStruct((N, WINDOW, KV_W), F32),
        jax.ShapeDtypeStruct((N, SC_K - 1, D_MODEL), F32), jax.ShapeDtypeStruct((N, CF_K - 1, D_MODEL), F32),
    ]
    scratch = [
        pltpu.VMEM((M, D_MODEL), BF16),
        pltpu.VMEM((M, Q_W), BF16),
        pltpu.VMEM((4, G, WINDOW + R, LANES), BF16),
        pltpu.VMEM((4, G, WINDOW + R, LANES), BF16),
        pltpu.VMEM((G, SC_HIST_PAD + R, D_MODEL), F32),
        pltpu.VMEM((G, CF_HIST_PAD + R, D_MODEL), F32),
        pltpu.VMEM((M, D_MODEL), BF16),
        pltpu.VMEM((M, D_MODEL), BF16),
        pltpu.VMEM((M, D_MODEL), BF16),
        pltpu.VMEM((M, D_MODEL), F32),
        pltpu.VMEM((M, D_MODEL), F32),
        pltpu.VMEM((3, M, D_MODEL), F32),
        pltpu.VMEM((M, D_MODEL), BF16),
    ]
    cf_rb = max(rb for rb in CF_ROW_BLOCKS if R % rb == 0)
    kern = functools.partial(_mixer_kernel, G=G, R=R, NT=NT, lead=lead, cf_rb=cf_rb)
    return pl.pallas_call(
        kern, grid=grid, in_specs=in_specs, out_specs=out_specs, out_shape=out_shape, scratch_shapes=scratch,
        compiler_params=pltpu.CompilerParams(dimension_semantics=("arbitrary", "arbitrary"),
                                             vmem_limit_bytes=VMEM_LIMIT_BYTES),
        name="mixer",
    )(p["attn_sinks"][layer], x, *tabs, kst, vst, scst, cfst,
      p["norm1_g"], p["w_in"], p["b_gate"], p["w_attn_out"], p["sc_conv_w"], p["w_sc_out"],
      p["cf_conv_w"], p["cf_ln_g"], p["cf_ln_b"], p["w_cf_out"], p["w_o"])


def _ffn_call(layer, x, fst, p, *, G, R, final_norm):
    N, L, _ = x.shape
    NT = L // R
    M = G * R
    grid = (N // G, NT)
    row_spec = pl.BlockSpec((G, R, D_MODEL), lambda n, t: (n, t, 0))
    st_spec = pl.BlockSpec((G, FFN_K - 1, 2 * D_FF), lambda n, t: (n, 0, 0))
    in_specs = [
        row_spec, st_spec,
        _const_spec((1, D_MODEL), layer), _const_spec((D_MODEL, 2 * D_FF), layer),
        _const_spec((FFN_K, 2 * D_FF), layer), _const_spec((D_FF, D_MODEL), layer), _const_spec((1, D_MODEL), None),
    ]
    out_shape = [jax.ShapeDtypeStruct((N, L, D_MODEL), F32), jax.ShapeDtypeStruct((N, FFN_K - 1, 2 * D_FF), F32)]
    scratch = [
        pltpu.VMEM((M, D_MODEL), BF16),
        pltpu.VMEM((G, SC_HIST_PAD + R, 2 * D_FF), F32),
        pltpu.VMEM((M, D_FF), BF16),
    ]
    kern = functools.partial(_ffn_kernel, G=G, R=R, NT=NT, final_norm=final_norm)
    return pl.pallas_call(
        kern, grid=grid, in_specs=in_specs, out_specs=[row_spec, st_spec], out_shape=out_shape,
        scratch_shapes=scratch,
        compiler_params=pltpu.CompilerParams(dimension_semantics=("arbitrary", "arbitrary"),
                                             vmem_limit_bytes=VMEM_LIMIT_BYTES),
        name="ffn",
    )(x, fst, p["norm2_g"], p["w_up"], p["ffn_conv_w"], p["w_down"], p["final_g"])


def _rope_tables(pos):
    half = ROT_DIM // 2
    inv_freq = jnp.power(ROPE_THETA, -jnp.arange(0, ROT_DIM, 2, dtype=F32) / ROT_DIM)
    ang = pos.astype(F32)[:, None] * inv_freq[None, :]
    cos, sin = jnp.cos(ang), jnp.sin(ang)
    n = pos.shape[0]
    ones = jnp.ones((n, HEAD_DIM - ROT_DIM), F32)
    zeros = jnp.zeros((n, HEAD_DIM - ROT_DIM), F32)
    zh = jnp.zeros((n, half), F32)
    c = jnp.concatenate([cos, cos, ones], axis=1)
    sa = jnp.concatenate([zh, sin, zeros], axis=1)
    sb = jnp.concatenate([-sin, zh, zeros], axis=1)
    reps = LANES // HEAD_DIM
    return tuple(jnp.tile(a, (1, reps)) for a in (c, sa, sb))


def kernel(x_prompt, x_sample, cache_k, cache_v, state_sc, state_cf, state_ffn, meta_tokens, norm1_g, w_in, b_gate,
           attn_sinks, w_attn_out, sc_conv_w, w_sc_out, cf_conv_w, cf_ln_g, cf_ln_b, w_cf_out, w_o, norm2_g, w_up,
           ffn_conv_w, w_down, final_g):
    B, S, _ = x_prompt.shape
    DB, DS, _ = x_sample.shape
    depth = w_in.shape[0]
    lead = CHUNK - N_META
    LP = lead + N_META + S
    assert LP % PROMPT_TILE_ROWS == 0 and DS == CHUNK and DB % SAMPLE_SEQS_PER_TILE == 0
    assert cache_k.shape[2] == WINDOW

    p = dict(
        attn_sinks=attn_sinks,
        norm1_g=norm1_g[:, None, :], w_in=w_in.astype(BF16), b_gate=b_gate[:, None, :],
        w_attn_out=w_attn_out.astype(BF16), sc_conv_w=sc_conv_w, w_sc_out=w_sc_out.astype(BF16),
        cf_conv_w=cf_conv_w, cf_ln_g=cf_ln_g[:, None, :], cf_ln_b=cf_ln_b[:, None, :],
        w_cf_out=w_cf_out.astype(BF16), w_o=w_o.astype(BF16),
        norm2_g=norm2_g[:, None, :], w_up=w_up.astype(BF16), ffn_conv_w=ffn_conv_w, w_down=w_down.astype(BF16),
        final_g=final_g[None, :],
    )

    xp = jnp.concatenate([jnp.zeros((B, lead, D_MODEL), F32),
                          jnp.broadcast_to(meta_tokens[None], (B, N_META, D_MODEL)), x_prompt], axis=1)
    tabs_p = _rope_tables(jnp.arange(LP, dtype=jnp.int32) - lead)
    xs = x_sample
    tabs_s = _rope_tables(PAST_LEN + jnp.arange(DS, dtype=jnp.int32))

    zk = jnp.zeros((B, WINDOW, KV_W), F32)
    zsc = jnp.zeros((B, SC_K - 1, D_MODEL), F32)
    zcf = jnp.zeros((B, CF_K - 1, D_MODEL), F32)
    zffn = jnp.zeros((B, FFN_K - 1, 2 * D_FF), F32)
    ck = cache_k.reshape(depth, DB, WINDOW, KV_W)
    cv = cache_v.reshape(depth, DB, WINDOW, KV_W)

    st_p, st_s = [], []
    for i in range(depth):
        last = i == depth - 1
        xp, kp, vp, scp, cfp = _mixer_call(i, xp, tabs_p, zk, zk, zsc, zcf, p, G=1, R=PROMPT_TILE_ROWS, lead=lead)
        xp, fp = _ffn_call(i, xp, zffn, p, G=1, R=PROMPT_TILE_ROWS, final_norm=last)
        xs, ks, vs, scs, cfs = _mixer_call(i, xs, tabs_s, ck[i], cv[i], state_sc[i], state_cf[i], p,
                                           G=SAMPLE_SEQS_PER_TILE, R=CHUNK, lead=None)
        xs, fs = _ffn_call(i, xs, state_ffn[i], p, G=SAMPLE_SEQS_PER_TILE, R=CHUNK, final_norm=last)
        st_p.append((kp, vp, scp, cfp, fp))
        st_s.append((ks, vs, scs, cfs, fs))

    def stack(sts, j, n):
        a = jnp.stack([s[j] for s in sts])
        if j < 2:
            a = a.reshape(depth, n, WINDOW, N_KV_HEADS, HEAD_DIM)
        return a

    y_prompt = xp[:, lead + N_META:]
    return ((y_prompt, xs) + tuple(stack(st_p, j, B) for j in range(5)) + tuple(stack(st_s, j, DB) for j in range(5)))
```
